```python
import jax, jax.numpy as jnp
from jax import lax
import numpy as np

D_MODEL = 1024
BATCH = 32
SEQ = 2048
DEPTH = 2
DEC_BATCH = 16
DEC_SEQ = 32
PAST_LEN = 1024

CHUNK = 64
D_PLE = 256
D_MIX = D_MODEL
D_A = D_MIX // 4
H_A = 4
GMLP_CHUNK = 128
D_B = D_MIX // 4
CONV_W = 3
D_C = D_MIX // 2
H_C = 4
DV_C = D_C // H_C
DK_C = DV_C // 2
GATE_RANK = 16
GATE_TAU = 16.0
D_FF = 2816
EPS = 1e-6

SPLIT_SIZES = (D_A, D_A, D_B, D_B, D_B, H_C * DK_C, H_C * DK_C, D_C, D_C, GATE_RANK)
P_TOT = sum(SPLIT_SIZES)
SPLIT_IDX = tuple(int(s) for s in np.cumsum(SPLIT_SIZES)[:-1])

kernel_name = "hybrid_streaming_gmlp_conv_gla_step"


def rms_norm(x, g):
    xf = x.astype(jnp.float32)
    y = xf * lax.rsqrt(jnp.mean(xf * xf, axis=-1, keepdims=True) + EPS)
    return (y * g.astype(jnp.float32)).astype(x.dtype)


def swiglu(x, wg, wu, wd):
    return (jax.nn.silu(x @ wg) * (x @ wu)) @ wd


def block_causal_mask(L):
    i = np.arange(L)
    return ((i[None, :] // CHUNK) <= (i[:, None] // CHUNK)).astype(np.float32)


def gmlp_mix(u, v, w_s, b_s):
    bsz, t, _ = v.shape
    L = min(t, GMLP_CHUNK)
    vf = v.astype(jnp.float32)
    mu = jnp.mean(vf, axis=-1, keepdims=True)
    var = jnp.mean(jnp.square(vf - mu), axis=-1, keepdims=True)
    vn = (vf - mu) * lax.rsqrt(var + EPS)
    vh = vn.reshape(bsz, t // L, L, H_A, D_A // H_A)
    W = w_s[:, :L, :L].astype(jnp.float32) * block_causal_mask(L)
    bias = b_s[:, :L].astype(jnp.float32).T[None, None, :, :, None]
    s = jnp.einsum('hij,bnjhd->bnihd', W, vh) + bias
    y = (u.astype(jnp.float32) * s.reshape(bsz, t, D_A)).astype(u.dtype)
    return y, vn.astype(v.dtype)


def short_conv(b, c, xin, hist, conv_w):
    z = c * xin
    t = z.shape[1]
    zp = jnp.concatenate([hist.astype(z.dtype), z], axis=1)
    y = sum(conv_w[j] * zp[:, j:j + t] for j in range(CONV_W))
    return b * y, zp[:, -(CONV_W - 1):]


def gla_recurrence(q, k, v, g, s0, L):
    bsz, t, h, _ = q.shape
    n = t // L
    f32 = jnp.float32
    q = q.astype(f32).reshape(bsz, n, L, h, DK_C) * (DK_C ** -0.5)
    k = k.astype(f32).reshape(bsz, n, L, h, DK_C)
    v = v.astype(f32).reshape(bsz, n, L, h, DV_C)
    G = jnp.cumsum(g.astype(f32).reshape(bsz, n, L, h, DK_C), axis=2)
    G_last = G[:, :, -1:]
    q_t = q * jnp.exp(G)
    k_t = k * jnp.exp(-G)
    k_e = k * jnp.exp(G_last - G)
    A = jnp.einsum('bnihk,bnjhk->bnhij', q_t, k_t)
    A = jnp.where(np.tril(np.ones((L, L), dtype=bool)), A, 0.0)
    o_intra = jnp.einsum('bnhij,bnjhv->bnihv', A, v)
    dS = jnp.einsum('bnjhk,bnjhv->nbhkv', k_e, v)
    decay = jnp.exp(G_last[:, :, 0]).transpose(1, 0, 2, 3)

    def step(S, inp):
        a, d = inp
        return a[..., None] * S + d, S

    s_fin, s_prev = lax.scan(step, s0.astype(f32), (decay, dS))
    o_inter = jnp.einsum('bnihk,nbhkv->bnihv', q_t, s_prev)
    o = (o_intra + o_inter).reshape(bsz, t, h, DV_C)
    return o, s_fin


def gla_mixer(q, k, v, r, g_low, w_gate2, b_gate2, gla_norm, s0):
    bsz, t, _ = q.shape
    g = jax.nn.log_sigmoid((g_low @ w_gate2 + b_gate2).astype(jnp.float32)) / GATE_TAU
    o, s_fin = gla_recurrence(q.reshape(bsz, t, H_C, DK_C), k.reshape(bsz, t, H_C, DK_C),
                              v.reshape(bsz, t, H_C, DV_C), g.reshape(bsz, t, H_C, DK_C),
                              s0, min(t, CHUNK))
    o = o * lax.rsqrt(jnp.mean(o * o, axis=-1, keepdims=True) + EPS)
    o = o * gla_norm.astype(jnp.float32).reshape(H_C, DV_C)
    y = o.reshape(bsz, t, D_C).astype(r.dtype) * jax.nn.silu(r)
    return y, s_fin


def trunk_layer(x, p, conv_hist, gla_s0, lw):
    (n_f1, wg1, wu1, wd1, n_mix, w_in, w_s, b_s, conv_w, w_gate2, b_gate2, gla_norm, w_o,
     n_f2, wg2, wu2, wd2, n_ple, w_ple_gate, w_ple_proj) = lw
    x = x + 0.5 * swiglu(rms_norm(x, n_f1), wg1, wu1, wd1)
    h = rms_norm(x, n_mix) @ w_in
    u_a, v_a, b_b, c_b, x_b, q_c, k_c, v_c, r_c, g_low = jnp.split(h, SPLIT_IDX, axis=-1)
    y_a, v_rows = gmlp_mix(u_a, v_a, w_s, b_s)
    y_b, conv_new = short_conv(b_b, c_b, x_b, conv_hist, conv_w)
    y_c, s_new = gla_mixer(q_c, k_c, v_c, r_c, g_low, w_gate2, b_gate2, gla_norm, gla_s0)
    x = x + jnp.concatenate([y_a, y_b, y_c], axis=-1) @ w_o
    x = x + 0.5 * swiglu(rms_norm(x, n_f2), wg2, wu2, wd2)
    gate = jax.nn.sigmoid(rms_norm(x, n_ple) @ w_ple_gate)
    x = x + gate * (p.astype(x.dtype) @ w_ple_proj)
    return x, conv_new, s_new, v_rows


def setup_inputs(seed: int = 0) -> dict:
    key = jax.random.key(seed)
    ks = jax.random.split(key, 32)

    def nrm(k, shape, scale=1.0):
        return jax.random.normal(k, shape, jnp.float32) * scale

    def gain(k, shape):
        return 1.0 + 0.05 * jax.random.normal(k, shape, jnp.float32)

    return {
        "x_prompt": nrm(ks[0], (BATCH, SEQ, D_MODEL)),
        "x_sample": nrm(ks[1], (DEC_BATCH, DEC_SEQ, D_MODEL)),
        "p_prompt": nrm(ks[2], (DEPTH, BATCH, SEQ, D_PLE)),
        "p_sample": nrm(ks[3], (DEPTH, DEC_BATCH, DEC_SEQ, D_PLE)),
        "state_conv": nrm(ks[4], (DEPTH, DEC_BATCH, CONV_W - 1, D_B)),
        "state_gla": nrm(ks[5], (DEPTH, DEC_BATCH, H_C, DK_C, DV_C), 0.5),
        "norm_ffn1": gain(ks[6], (DEPTH, D_MODEL)),
        "w_ffn1_gate": nrm(ks[7], (DEPTH, D_MODEL, D_FF), D_MODEL ** -0.5),
        "w_ffn1_up": nrm(ks[8], (DEPTH, D_MODEL, D_FF), D_MODEL ** -0.5),
        "w_ffn1_down": nrm(ks[9], (DEPTH, D_FF, D_MODEL), D_FF ** -0.5),
        "norm_mix": gain(ks[10], (DEPTH, D_MODEL)),
        "w_in": nrm(ks[11], (DEPTH, D_MODEL, P_TOT), D_MODEL ** -0.5),
        "w_s": nrm(ks[12], (DEPTH, H_A, GMLP_CHUNK, GMLP_CHUNK), GMLP_CHUNK ** -0.5),
        "b_s": 1.0 + 0.01 * jax.random.normal(ks[13], (DEPTH, H_A, GMLP_CHUNK), jnp.float32),
        "conv_w": nrm(ks[14], (DEPTH, CONV_W, D_B), CONV_W ** -0.5),
        "w_gate2": nrm(ks[15], (DEPTH, GATE_RANK, H_C * DK_C), GATE_RANK ** -0.5),
        "b_gate2": nrm(ks[16], (DEPTH, H_C * DK_C), 0.1),
        "gla_norm": gain(ks[17], (DEPTH, D_C)),
        "w_o": nrm(ks[18], (DEPTH, D_MIX, D_MODEL), D_MIX ** -0.5),
        "norm_ffn2": gain(ks[19], (DEPTH, D_MODEL)),
        "w_ffn2_gate": nrm(ks[20], (DEPTH, D_MODEL, D_FF), D_MODEL ** -0.5),
        "w_ffn2_up": nrm(ks[21], (DEPTH, D_MODEL, D_FF), D_MODEL ** -0.5),
        "w_ffn2_down": nrm(ks[22], (DEPTH, D_FF, D_MODEL), D_FF ** -0.5),
        "norm_ple": gain(ks[23], (DEPTH, D_MODEL)),
        "w_ple_gate": nrm(ks[24], (DEPTH, D_MODEL, D_MODEL), D_MODEL ** -0.5),
        "w_ple_proj": nrm(ks[25], (DEPTH, D_PLE, D_MODEL), D_PLE ** -0.5),
        "final_norm": gain(ks[26], (D_MODEL,)),
    }


def reference(x_prompt, x_sample, p_prompt, p_sample, state_conv, state_gla,
              norm_ffn1, w_ffn1_gate, w_ffn1_up, w_ffn1_down, norm_mix, w_in, w_s, b_s,
              conv_w, w_gate2, b_gate2, gla_norm, w_o, norm_ffn2, w_ffn2_gate, w_ffn2_up,
              w_ffn2_down, norm_ple, w_ple_gate, w_ple_proj, final_norm):
    xp, xs = x_prompt, x_sample
    bp = x_prompt.shape[0]
    conv_hist_p = jnp.zeros((bp, CONV_W - 1, D_B), x_prompt.dtype)
    gla_s0_p = jnp.zeros((bp, H_C, DK_C, DV_C), jnp.float32)
    conv_p, gla_p, conv_s, gla_s, vrow_s = [], [], [], [], []
    for i in range(DEPTH):
        lw = (norm_ffn1[i], w_ffn1_gate[i], w_ffn1_up[i], w_ffn1_down[i], norm_mix[i], w_in[i],
              w_s[i], b_s[i], conv_w[i], w_gate2[i], b_gate2[i], gla_norm[i], w_o[i],
              norm_ffn2[i], w_ffn2_gate[i], w_ffn2_up[i], w_ffn2_down[i], norm_ple[i],
              w_ple_gate[i], w_ple_proj[i])
        xp, cp, sp, _ = trunk_layer(xp, p_prompt[i], conv_hist_p, gla_s0_p, lw)
        xs, cs, ss, vs = trunk_layer(xs, p_sample[i], state_conv[i], state_gla[i], lw)
        conv_p.append(cp)
        gla_p.append(sp)
        conv_s.append(cs)
        gla_s.append(ss)
        vrow_s.append(vs)
    y_prompt = rms_norm(xp, final_norm)
    y_sample = rms_norm(xs, final_norm)
    conv_prompt = jnp.stack(conv_p)
    gla_prompt = jnp.stack(gla_p)
    conv_sample = jnp.stack(conv_s)
    gla_sample = jnp.stack(gla_s)
    gmlp_v_sample = jnp.stack(vrow_s)
    return (y_prompt, y_sample, conv_prompt, gla_prompt, conv_sample, gla_sample, gmlp_v_sample)
```

```python
import functools

import jax
import jax.numpy as jnp
from jax import lax
from jax.experimental import pallas as pl
from jax.experimental.pallas import tpu as pltpu

F32 = jnp.float32
BF16 = jnp.bfloat16

D_MODEL = 1024
D_FF = 2816
D_PLE = 256
D_A = 256
H_A = 4
GMLP_CHUNK = 128
CHUNK = 64
D_B = 256
CONV_W = 3
H_C = 4
DK_C = 64
DV_C = 128
D_QK = H_C * DK_C
D_C = H_C * DV_C
GATE_RANK = 16
GATE_TAU = 16.0
EPS = 1e-6

OFF_U, OFF_V, OFF_BB, OFF_CB, OFF_XB = 0, 256, 512, 768, 1024
OFF_Q, OFF_K, OFF_VC, OFF_R, OFF_GL = 1280, 1536, 1792, 2304, 2816
LANE = 128
P_PAD = OFF_GL + LANE
FF_CHUNK = 256
N_FF = D_FF // FF_CHUNK
HIST_ROW = 8
VMEM_LIMIT = 60 * 1024 * 1024


def _rms(x, g):
    ms = jnp.mean(x * x, axis=-1, keepdims=True)
    return (x * lax.rsqrt(ms + EPS)) * g


def _silu(x):
    return x * jax.nn.sigmoid(x)


def _log_sigmoid(z):
    return -(jnp.maximum(-z, 0.0) + jnp.log1p(jnp.exp(-jnp.abs(z))))


def _ffn_residual(x, norm_ref, wgu_ref, wd_ref, xn_ref, acc_ref):
    xn_ref[...] = _rms(x, norm_ref[...]).astype(BF16)
    acc_ref[...] = jnp.zeros_like(acc_ref)

    def body(c, carry):
        gu = jnp.dot(xn_ref[...], wgu_ref[c], preferred_element_type=F32)
        a = (_silu(gu[:, :FF_CHUNK]) * gu[:, FF_CHUNK:]).astype(BF16)
        acc_ref[...] += jnp.dot(a, wd_ref[c], preferred_element_type=F32)
        return carry

    lax.fori_loop(0, N_FF, body, 0)
    return x + 0.5 * acc_ref[...]


def _head_blocks(x, n_heads, width):
    head = lax.broadcasted_iota(jnp.int32, x.shape, 1) // width
    return jnp.concatenate([jnp.where(head == h, x, 0.0) for h in range(n_heads)], axis=0)


def _mix_kernel(*refs, seg_len, n_seg, nt, has_init, emit_vrows):
    it = iter(refs)
    x_ref = next(it)
    conv0_ref = next(it) if has_init else None
    gla0_ref = next(it) if has_init else None
    n1_ref, wgu_ref, wd_ref, nmix_ref, win_ref = (next(it) for _ in range(5))
    wcat_ref, bias_ref, cw_ref, wg2_ref, bg2_ref, gn_ref, wo_ref = (next(it) for _ in range(7))
    xo_ref, convo_ref, glao_ref = next(it), next(it), next(it)
    vrows_ref = next(it) if emit_vrows else None
    xn_ref, acc_ref, h_ref, y_ref, g_ref, st_ref, zp_ref = (next(it) for _ in range(7))

    t = pl.program_id(1)
    lg = min(seg_len, GMLP_CHUNK)
    lb = min(seg_len, CHUNK)

    x1 = _ffn_residual(x_ref[...], n1_ref, wgu_ref, wd_ref, xn_ref, acc_ref)
    xn_ref[...] = _rms(x1, nmix_ref[...]).astype(BF16)
    h_ref[...] = jnp.dot(xn_ref[...], win_ref[...], preferred_element_type=F32)

    z = jnp.dot(h_ref[:, OFF_GL:OFF_GL + LANE].astype(BF16), wg2_ref[...], preferred_element_type=F32)
    g_ref[...] = _log_sigmoid(z + bg2_ref[...]) * (1.0 / GATE_TAU)

    wi = lax.broadcasted_iota(jnp.int32, (lg, H_A * lg), 0)
    wj = lax.broadcasted_iota(jnp.int32, (lg, H_A * lg), 1) % lg
    wmask = jnp.where(wj // CHUNK <= wi // CHUNK, wcat_ref[...], 0.0).astype(BF16)

    tri = (lax.broadcasted_iota(jnp.int32, (lb, lb), 1) <= lax.broadcasted_iota(jnp.int32, (lb, lb), 0)).astype(F32)
    ai = lax.broadcasted_iota(jnp.int32, (lb, H_C * lb), 0)
    aj = lax.broadcasted_iota(jnp.int32, (lb, H_C * lb), 1) % lb
    causal = aj <= ai
    sr = lax.broadcasted_iota(jnp.int32, (D_C, D_QK), 0) // DV_C
    sc = lax.broadcasted_iota(jnp.int32, (D_C, D_QK), 1) // DK_C
    st_diag = sr == sc

    def segment(s, carry):
        r0 = pl.multiple_of(s * seg_len, seg_len)

        if has_init:
            zp_ref[HIST_ROW - 2:HIST_ROW, :] = conv0_ref[s]
            sbd = jnp.concatenate(
                [jnp.concatenate([gla0_ref[s, h] if hh == h else jnp.zeros((DK_C, DV_C), F32)
                                  for hh in range(H_C)], axis=1) for h in range(H_C)], axis=0)
            st_ref[...] = sbd.T
        else:
            @pl.when(t == 0)
            def _():
                zp_ref[HIST_ROW - 2:HIST_ROW, :] = jnp.zeros((CONV_W - 1, D_B), F32)
                st_ref[...] = jnp.zeros_like(st_ref)

        def gmlp_chunk(c, carry):
            rows = pl.ds(pl.multiple_of(r0 + c * lg, lg), lg)
            v = h_ref[rows, OFF_V:OFF_V + D_A]
            mu = jnp.mean(v, axis=-1, keepdims=True)
            vc = v - mu
            var = jnp.mean(vc * vc, axis=-1, keepdims=True)
            vn = vc * lax.rsqrt(var + EPS)
            if emit_vrows:
                vrows_ref[rows, :] = vn
            vbd = _head_blocks(vn, H_A, D_A // H_A).astype(BF16)
            sp = jnp.dot(wmask, vbd, preferred_element_type=F32) + bias_ref[...]
            y_ref[rows, 0:D_A] = (h_ref[rows, OFF_U:OFF_U + D_A] * sp).astype(BF16)
            return carry

        lax.fori_loop(0, seg_len // lg, gmlp_chunk, 0)

        seg = pl.ds(r0, seg_len)
        zc = h_ref[seg, OFF_CB:OFF_CB + D_B] * h_ref[seg, OFF_XB:OFF_XB + D_B]
        zp_ref[HIST_ROW:HIST_ROW + seg_len, :] = zc
        yb = (cw_ref[0:1, :] * zp_ref[HIST_ROW - 2:HIST_ROW - 2 + seg_len, :]
              + cw_ref[1:2, :] * zp_ref[HIST_ROW - 1:HIST_ROW - 1 + seg_len, :]
              + cw_ref[2:3, :] * zc)
        y_ref[seg, D_A:D_A + D_B] = (h_ref[seg, OFF_BB:OFF_BB + D_B] * yb).astype(BF16)
        hist = zp_ref[HIST_ROW + seg_len - 2:HIST_ROW + seg_len, :]
        zp_ref[HIST_ROW - 2:HIST_ROW, :] = hist
        convo_ref[s] = hist

        def gla_block(b, carry):
            rows = pl.ds(pl.multiple_of(r0 + b * lb, lb), lb)
            gcum = jnp.dot(tri, g_ref[rows, :], preferred_element_type=F32, precision=lax.Precision.HIGHEST)
            glast = gcum[lb - 1:lb, :]
            q = h_ref[rows, OFF_Q:OFF_Q + D_QK] * (DK_C ** -0.5)
            k = h_ref[rows, OFF_K:OFF_K + D_QK]
            v = h_ref[rows, OFF_VC:OFF_VC + D_C]
            q_t = (q * jnp.exp(gcum)).astype(BF16)
            k_t = k * jnp.exp(-gcum)
            k_e = (k * jnp.exp(glast - gcum)).astype(BF16)
            kbd = _head_blocks(k_t, H_C, DK_C).astype(BF16)
            a = lax.dot_general(q_t, kbd, (((1,), (1,)), ((), ())), preferred_element_type=F32)
            a = jnp.where(causal, a, 0.0).astype(BF16)
            vbd = _head_blocks(v, H_C, DV_C).astype(BF16)
            st = st_ref[...]
            o = jnp.dot(a, vbd, preferred_element_type=F32)
            o = o + lax.dot_general(q_t, st.astype(BF16), (((1,), (1,)), ((), ())), preferred_element_type=F32)
            ds = lax.dot_general(v.astype(BF16), k_e, (((0,), (0,)), ((), ())), preferred_element_type=F32)
            st_ref[...] = st * jnp.exp(glast) + jnp.where(st_diag, ds, 0.0)
            r = h_ref[rows, OFF_R:OFF_R + D_C]
            for h in range(H_C):
                cols = slice(h * DV_C, (h + 1) * DV_C)
                oh = o[:, cols]
                oh = oh * lax.rsqrt(jnp.mean(oh * oh, axis=-1, keepdims=True) + EPS)
                oh = oh * gn_ref[:, cols]
                y_ref[rows, D_A + D_B + h * DV_C:D_A + D_B + (h + 1) * DV_C] = (oh * _silu(r[:, cols])).astype(BF16)
            return carry

        lax.fori_loop(0, seg_len // lb, gla_block, 0)

        @pl.when(t == nt - 1)
        def _():
            sbd = st_ref[...].T
            for h in range(H_C):
                glao_ref[s, h] = sbd[h * DK_C:(h + 1) * DK_C, h * DV_C:(h + 1) * DV_C]

        return carry

    lax.fori_loop(0, n_seg, segment, 0)

    xo_ref[...] = x1 + jnp.dot(y_ref[...], wo_ref[...], preferred_element_type=F32)


def _ple_kernel(x_ref, p_ref, n2_ref, wgu_ref, wd_ref, nple_ref, wpg_ref, wpp_ref, fin_ref, xo_ref,
                xn_ref, acc_ref, *, final):
    x3 = _ffn_residual(x_ref[...], n2_ref, wgu_ref, wd_ref, xn_ref, acc_ref)
    xn = _rms(x3, nple_ref[...]).astype(BF16)
    gate = jax.nn.sigmoid(jnp.dot(xn, wpg_ref[...], preferred_element_type=F32))
    proj = jnp.dot(p_ref[...].astype(BF16), wpp_ref[...], preferred_element_type=F32)
    x4 = x3 + gate * proj
    xo_ref[...] = _rms(x4, fin_ref[...]) if final else x4


def _const_spec(shape):
    nd = len(shape)
    return pl.BlockSpec(shape, lambda b, t: (0,) * nd, pipeline_mode=pl.Buffered(1))


def _mix_call(x, conv0, gla0, w, *, seg_len, n_seg, emit_vrows):
    nb, tlen, _ = x.shape
    tm = seg_len * n_seg
    nt = tlen // tm
    nseq = nb * n_seg if n_seg > 1 else nb
    has_init = conv0 is not None
    lg = min(seg_len, GMLP_CHUNK)

    x_spec = pl.BlockSpec((None, tm, D_MODEL), lambda b, t: (b, t, 0))
    state_specs = [pl.BlockSpec((n_seg, CONV_W - 1, D_B), lambda b, t: (b, 0, 0)),
                   pl.BlockSpec((n_seg, H_C, DK_C, DV_C), lambda b, t: (b, 0, 0, 0))]
    weights = [w["n1"], w["wgu1"], w["wd1"], w["nmix"], w["win"], w["wcat"],
               w["bias"], w["cw"], w["wg2"], w["bg2"], w["gn"], w["wo"]]
    in_specs = [x_spec] + (state_specs if has_init else []) + [_const_spec(a.shape) for a in weights]
    inputs = [x] + ([conv0, gla0] if has_init else []) + weights

    out_shape = [jax.ShapeDtypeStruct(x.shape, F32),
                 jax.ShapeDtypeStruct((nseq, CONV_W - 1, D_B), F32),
                 jax.ShapeDtypeStruct((nseq, H_C, DK_C, DV_C), F32)]
    out_specs = [x_spec] + state_specs
    if emit_vrows:
        out_shape.append(jax.ShapeDtypeStruct((nb, tlen, D_A), F32))
        out_specs.append(pl.BlockSpec((None, tm, D_A), lambda b, t: (b, t, 0)))

    scratch = [pltpu.VMEM((tm, D_MODEL), BF16),
               pltpu.VMEM((tm, D_MODEL), F32),
               pltpu.VMEM((tm, P_PAD), F32),
               pltpu.VMEM((tm, D_MODEL), BF16),
               pltpu.VMEM((tm, D_QK), F32),
               pltpu.VMEM((D_C, D_QK), F32),
               pltpu.VMEM((HIST_ROW + seg_len, D_B), F32)]
    kern = functools.partial(_mix_kernel, seg_len=seg_len, n_seg=n_seg, nt=nt, has_init=has_init,
                             emit_vrows=emit_vrows)
    return pl.pallas_call(
        kern, grid=(nb, nt), in_specs=in_specs, out_specs=out_specs, out_shape=out_shape,
        scratch_shapes=scratch, name="mix_layer",
        compiler_params=pltpu.CompilerParams(dimension_semantics=("arbitrary", "arbitrary"),
                                             vmem_limit_bytes=VMEM_LIMIT),
    )(*inputs)


def _ple_call(x, p, w, final_norm, *, tm, final):
    nb, tlen, _ = x.shape
    nt = tlen // tm
    x_spec = pl.BlockSpec((None, tm, D_MODEL), lambda b, t: (b, t, 0))
    p_spec = pl.BlockSpec((None, tm, D_PLE), lambda b, t: (b, t, 0))
    weights = [w["n2"], w["wgu2"], w["wd2"], w["nple"], w["wpg"], w["wpp"], final_norm]
    return pl.pallas_call(
        functools.partial(_ple_kernel, final=final), grid=(nb, nt),
        in_specs=[x_spec, p_spec] + [_const_spec(a.shape) for a in weights],
        out_specs=x_spec, out_shape=jax.ShapeDtypeStruct(x.shape, F32),
        scratch_shapes=[pltpu.VMEM((tm, D_MODEL), BF16), pltpu.VMEM((tm, D_MODEL), F32)],
        name="ple_layer",
        compiler_params=pltpu.CompilerParams(dimension_semantics=("arbitrary", "arbitrary"),
                                             vmem_limit_bytes=VMEM_LIMIT),
    )(x, p, *weights)


def _chunk_ffn(wg, wu, wd):
    g = wg.reshape(D_MODEL, N_FF, FF_CHUNK).transpose(1, 0, 2)
    u = wu.reshape(D_MODEL, N_FF, FF_CHUNK).transpose(1, 0, 2)
    return (jnp.concatenate([g, u], axis=-1).astype(BF16),
            wd.reshape(N_FF, FF_CHUNK, D_MODEL).astype(BF16))


def _layer_weights(i, lg, norm_ffn1, w_ffn1_gate, w_ffn1_up, w_ffn1_down, norm_mix, w_in, w_s, b_s, conv_w,
                   w_gate2, b_gate2, gla_norm, w_o, norm_ffn2, w_ffn2_gate, w_ffn2_up, w_ffn2_down, norm_ple,
                   w_ple_gate, w_ple_proj):
    wgu1, wd1 = _chunk_ffn(w_ffn1_gate[i], w_ffn1_up[i], w_ffn1_down[i])
    wgu2, wd2 = _chunk_ffn(w_ffn2_gate[i], w_ffn2_up[i], w_ffn2_down[i])
    row = lambda a: a.reshape(1, -1)
    return {
        "n1": row(norm_ffn1[i]), "wgu1": wgu1, "wd1": wd1, "nmix": row(norm_mix[i]),
        "win": jnp.pad(w_in[i], ((0, 0), (0, P_PAD - w_in.shape[-1]))).astype(BF16),
        "wcat": w_s[i, :, :lg, :lg].transpose(1, 0, 2).reshape(lg, H_A * lg),
        "bias": jnp.repeat(b_s[i, :, :lg].T, D_A // H_A, axis=1),
        "cw": conv_w[i],
        "wg2": jnp.pad(w_gate2[i], ((0, LANE - GATE_RANK), (0, 0))).astype(BF16),
        "bg2": row(b_gate2[i]), "gn": row(gla_norm[i]), "wo": w_o[i].astype(BF16),
        "n2": row(norm_ffn2[i]), "wgu2": wgu2, "wd2": wd2, "nple": row(norm_ple[i]),
        "wpg": w_ple_gate[i].astype(BF16), "wpp": w_ple_proj[i].astype(BF16),
    }


TM_PROMPT = 512


def kernel(x_prompt, x_sample, p_prompt, p_sample, state_conv, state_gla, norm_ffn1, w_ffn1_gate, w_ffn1_up,
           w_ffn1_down, norm_mix, w_in, w_s, b_s, conv_w, w_gate2, b_gate2, gla_norm, w_o, norm_ffn2,
           w_ffn2_gate, w_ffn2_up, w_ffn2_down, norm_ple, w_ple_gate, w_ple_proj, final_norm):
    depth = w_in.shape[0]
    dec_b, dec_t, _ = x_sample.shape
    params = (norm_ffn1, w_ffn1_gate, w_ffn1_up, w_ffn1_down, norm_mix, w_in, w_s, b_s, conv_w, w_gate2, b_gate2,
              gla_norm, w_o, norm_ffn2, w_ffn2_gate, w_ffn2_up, w_ffn2_down, norm_ple, w_ple_gate, w_ple_proj)
    fin = final_norm.reshape(1, -1)

    xp = x_prompt
    xs = x_sample.reshape(1, dec_b * dec_t, D_MODEL)
    ps = p_sample.reshape(depth, 1, dec_b * dec_t, D_PLE)
    conv_p, gla_p, conv_s, gla_s, vrow_s = [], [], [], [], []
    for i in range(depth):
        last = i == depth - 1
        wp = _layer_weights(i, min(TM_PROMPT, GMLP_CHUNK), *params)
        ws = _layer_weights(i, min(dec_t, GMLP_CHUNK), *params)

        xp, cp, sp = _mix_call(xp, None, None, wp, seg_len=TM_PROMPT, n_seg=1, emit_vrows=False)
        xp = _ple_call(xp, p_prompt[i], wp, fin, tm=TM_PROMPT, final=last)

        xs, cs, ss, vs = _mix_call(xs, state_conv[i], state_gla[i], ws, seg_len=dec_t, n_seg=dec_b,
                                   emit_vrows=True)
        xs = _ple_call(xs, ps[i], ws, fin, tm=dec_b * dec_t, final=last)

        conv_p.append(cp)
        gla_p.append(sp)
        conv_s.append(cs)
        gla_s.append(ss)
        vrow_s.append(vs.reshape(dec_b, dec_t, D_A))

    return (xp, xs.reshape(x_sample.shape), jnp.stack(conv_p), jnp.stack(gla_p), jnp.stack(conv_s),
            jnp.stack(gla_s), jnp.stack(vrow_s))
```

```python
import functools

import jax
import jax.numpy as jnp
from jax import lax
from jax.experimental import pallas as pl
from jax.experimental.pallas import tpu as pltpu

F32 = jnp.float32
BF16 = jnp.bfloat16

D_MODEL = 1024
D_FF = 2816
D_PLE = 256
D_A = 256
H_A = 4
GMLP_CHUNK = 128
CHUNK = 64
D_B = 256
CONV_W = 3
H_C = 4
DK_C = 64
DV_C = 128
D_QK = H_C * DK_C
D_C = H_C * DV_C
GATE_RANK = 16
GATE_TAU = 16.0
EPS = 1e-6

OFF_U, OFF_V, OFF_BB, OFF_CB, OFF_XB = 0, 256, 512, 768, 1024
OFF_Q, OFF_K, OFF_VC, OFF_R, OFF_GL = 1280, 1536, 1792, 2304, 2816
LANE = 128
P_PAD = OFF_GL + LANE
FF_CHUNK = 256
N_FF = D_FF // FF_CHUNK
HIST_ROW = 8
VMEM_LIMIT = 60 * 1024 * 1024


def _rms(x, g):
    ms = jnp.mean(x * x, axis=-1, keepdims=True)
    return (x * lax.rsqrt(ms + EPS)) * g


def _silu(x):
    return x * jax.nn.sigmoid(x)


def _log_sigmoid(z):
    return -(jnp.maximum(-z, 0.0) + jnp.log1p(jnp.exp(-jnp.abs(z))))


def _ffn_residual(x, norm_ref, wgu_ref, wd_ref, xn_ref, acc_ref):
    xn_ref[...] = _rms(x, norm_ref[...]).astype(BF16)
    acc_ref[...] = jnp.zeros_like(acc_ref)

    def body(c, carry):
        gu = jnp.dot(xn_ref[...], wgu_ref[c], preferred_element_type=F32)
        a = (_silu(gu[:, :FF_CHUNK]) * gu[:, FF_CHUNK:]).astype(BF16)
        acc_ref[...] += jnp.dot(a, wd_ref[c], preferred_element_type=F32)
        return carry

    lax.fori_loop(0, N_FF, body, 0, unroll=True)
    return x + 0.5 * acc_ref[...]


def _head_blocks(x, n_heads, width):
    head = lax.broadcasted_iota(jnp.int32, x.shape, 1) // width
    return jnp.concatenate([jnp.where(head == h, x, 0.0) for h in range(n_heads)], axis=0)


def _mix_kernel(*refs, seg_len, n_seg, nt, has_init, emit_vrows):
    it = iter(refs)
    x_ref = next(it)
    conv0_ref = next(it) if has_init else None
    gla0_ref = next(it) if has_init else None
    n1_ref, wgu_ref, wd_ref, nmix_ref, win_ref = (next(it) for _ in range(5))
    wcat_ref, bias_ref, cw_ref, wg2_ref, bg2_ref, gn_ref, wo_ref = (next(it) for _ in range(7))
    xo_ref, convo_ref, glao_ref = next(it), next(it), next(it)
    vrows_ref = next(it) if emit_vrows else None
    xn_ref, acc_ref, h_ref, y_ref, g_ref, st_ref, zp_ref, tri_ref = (next(it) for _ in range(8))

    t = pl.program_id(1)
    lg = min(seg_len, GMLP_CHUNK)
    lb = min(seg_len, CHUNK)

    x1 = _ffn_residual(x_ref[...], n1_ref, wgu_ref, wd_ref, xn_ref, acc_ref)
    xn_ref[...] = _rms(x1, nmix_ref[...]).astype(BF16)
    h_ref[...] = jnp.dot(xn_ref[...], win_ref[...], preferred_element_type=F32)

    z = jnp.dot(h_ref[:, OFF_GL:OFF_GL + LANE].astype(BF16), wg2_ref[...], preferred_element_type=F32)
    g = _log_sigmoid(z + bg2_ref[...]) * (1.0 / GATE_TAU)

    @pl.when((pl.program_id(0) == 0) & (t == 0))
    def _():
        ri = lax.broadcasted_iota(jnp.int32, tri_ref.shape, 0)
        ci = lax.broadcasted_iota(jnp.int32, tri_ref.shape, 1)
        tri_ref[...] = ((ci <= ri) & (ci // lb == ri // lb)).astype(BF16)

    g_hi = g.astype(BF16)
    rem = g - g_hi.astype(F32)
    g_mid = rem.astype(BF16)
    g_lo = (rem - g_mid.astype(F32)).astype(BF16)
    gsum = jnp.dot(tri_ref[...], jnp.concatenate([g_hi, g_mid, g_lo], axis=1), preferred_element_type=F32)
    g_ref[...] = (gsum[:, :D_QK] + gsum[:, D_QK:2 * D_QK]) + gsum[:, 2 * D_QK:]

    wi = lax.broadcasted_iota(jnp.int32, (lg, H_A * lg), 0)
    wj = lax.broadcasted_iota(jnp.int32, (lg, H_A * lg), 1) % lg
    wmask = jnp.where(wj // CHUNK <= wi // CHUNK, wcat_ref[...], 0.0).astype(BF16)

    ai = lax.broadcasted_iota(jnp.int32, (lb, H_C * lb), 0)
    aj = lax.broadcasted_iota(jnp.int32, (lb, H_C * lb), 1) % lb
    causal = aj <= ai
    sr = lax.broadcasted_iota(jnp.int32, (D_C, D_QK), 0) // DV_C
    sc = lax.broadcasted_iota(jnp.int32, (D_C, D_QK), 1) // DK_C
    st_diag = sr == sc

    def segment(s, carry):
        r0 = pl.multiple_of(s * seg_len, seg_len)

        if has_init:
            zp_ref[HIST_ROW - 2:HIST_ROW, :] = conv0_ref[s]
            sbd = jnp.concatenate(
                [jnp.concatenate([gla0_ref[s, h] if hh == h else jnp.zeros((DK_C, DV_C), F32)
                                  for hh in range(H_C)], axis=1) for h in range(H_C)], axis=0)
            st_ref[...] = sbd.T
        else:
            @pl.when(t == 0)
            def _():
                zp_ref[HIST_ROW - 2:HIST_ROW, :] = jnp.zeros((CONV_W - 1, D_B), F32)
                st_ref[...] = jnp.zeros_like(st_ref)

        def gmlp_chunk(c, carry):
            rows = pl.ds(pl.multiple_of(r0 + c * lg, lg), lg)
            v = h_ref[rows, OFF_V:OFF_V + D_A]
            mu = jnp.mean(v, axis=-1, keepdims=True)
            vc = v - mu
            var = jnp.mean(vc * vc, axis=-1, keepdims=True)
            vn = vc * lax.rsqrt(var + EPS)
            if emit_vrows:
                vrows_ref[rows, :] = vn
            vbd = _head_blocks(vn, H_A, D_A // H_A).astype(BF16)
            sp = jnp.dot(wmask, vbd, preferred_element_type=F32) + bias_ref[...]
            y_ref[rows, 0:D_A] = (h_ref[rows, OFF_U:OFF_U + D_A] * sp).astype(BF16)
            return carry

        lax.fori_loop(0, seg_len // lg, gmlp_chunk, 0, unroll=True)

        seg = pl.ds(r0, seg_len)
        zc = h_ref[seg, OFF_CB:OFF_CB + D_B] * h_ref[seg, OFF_XB:OFF_XB + D_B]
        zp_ref[HIST_ROW:HIST_ROW + seg_len, :] = zc
        yb = (cw_ref[0:1, :] * zp_ref[HIST_ROW - 2:HIST_ROW - 2 + seg_len, :]
              + cw_ref[1:2, :] * zp_ref[HIST_ROW - 1:HIST_ROW - 1 + seg_len, :]
              + cw_ref[2:3, :] * zc)
        y_ref[seg, D_A:D_A + D_B] = (h_ref[seg, OFF_BB:OFF_BB + D_B] * yb).astype(BF16)
        hist = zp_ref[HIST_ROW + seg_len - 2:HIST_ROW + seg_len, :]
        zp_ref[HIST_ROW - 2:HIST_ROW, :] = hist
        convo_ref[s] = hist

        def gla_block(b, carry):
            rows = pl.ds(pl.multiple_of(r0 + b * lb, lb), lb)
            gcum = g_ref[rows, :]
            glast = gcum[lb - 1:lb, :]
            q = h_ref[rows, OFF_Q:OFF_Q + D_QK] * (DK_C ** -0.5)
            k = h_ref[rows, OFF_K:OFF_K + D_QK]
            v = h_ref[rows, OFF_VC:OFF_VC + D_C]
            q_t = (q * jnp.exp(gcum)).astype(BF16)
            k_t = k * jnp.exp(-gcum)
            k_e = (k * jnp.exp(glast - gcum)).astype(BF16)
            kbd = _head_blocks(k_t, H_C, DK_C).astype(BF16)
            a = lax.dot_general(q_t, kbd, (((1,), (1,)), ((), ())), preferred_element_type=F32)
            a = jnp.where(causal, a, 0.0).astype(BF16)
            vbd = _head_blocks(v, H_C, DV_C).astype(BF16)
            st = st_ref[...]
            o = jnp.dot(a, vbd, preferred_element_type=F32)
            o = o + lax.dot_general(q_t, st.astype(BF16), (((1,), (1,)), ((), ())), preferred_element_type=F32)
            ds = lax.dot_general(v.astype(BF16), k_e, (((0,), (0,)), ((), ())), preferred_element_type=F32)
            st_ref[...] = st * jnp.exp(glast) + jnp.where(st_diag, ds, 0.0)
            r = h_ref[rows, OFF_R:OFF_R + D_C]
            for h in range(H_C):
                cols = slice(h * DV_C, (h + 1) * DV_C)
                oh = o[:, cols]
                oh = oh * lax.rsqrt(jnp.mean(oh * oh, axis=-1, keepdims=True) + EPS)
                oh = oh * gn_ref[:, cols]
                y_ref[rows, D_A + D_B + h * DV_C:D_A + D_B + (h + 1) * DV_C] = (oh * _silu(r[:, cols])).astype(BF16)
            return carry

        lax.fori_loop(0, seg_len // lb, gla_block, 0, unroll=True)

        @pl.when(t == nt - 1)
        def _():
            sbd = st_ref[...].T
            for h in range(H_C):
                glao_ref[s, h] = sbd[h * DK_C:(h + 1) * DK_C, h * DV_C:(h + 1) * DV_C]

        return carry

    lax.fori_loop(0, n_seg, segment, 0)

    xo_ref[...] = x1 + jnp.dot(y_ref[...], wo_ref[...], preferred_element_type=F32)


def _ple_kernel(x_ref, p_ref, n2_ref, wgu_ref, wd_ref, nple_ref, wpg_ref, wpp_ref, fin_ref, xo_ref,
                xn_ref, acc_ref, *, final):
    x3 = _ffn_residual(x_ref[...], n2_ref, wgu_ref, wd_ref, xn_ref, acc_ref)
    xn = _rms(x3, nple_ref[...]).astype(BF16)
    gate = jax.nn.sigmoid(jnp.dot(xn, wpg_ref[...], preferred_element_type=F32))
    proj = jnp.dot(p_ref[...].astype(BF16), wpp_ref[...], preferred_element_type=F32)
    x4 = x3 + gate * proj
    xo_ref[...] = _rms(x4, fin_ref[...]) if final else x4


def _const_spec(shape):
    nd = len(shape)
    return pl.BlockSpec(shape, lambda b, t: (0,) * nd, pipeline_mode=pl.Buffered(1))


def _mix_call(x, conv0, gla0, w, *, seg_len, n_seg, emit_vrows):
    nb, tlen, _ = x.shape
    tm = seg_len * n_seg
    nt = tlen // tm
    nseq = nb * n_seg if n_seg > 1 else nb
    has_init = conv0 is not None
    lg = min(seg_len, GMLP_CHUNK)

    x_spec = pl.BlockSpec((None, tm, D_MODEL), lambda b, t: (b, t, 0))
    state_specs = [pl.BlockSpec((n_seg, CONV_W - 1, D_B), lambda b, t: (b, 0, 0)),
                   pl.BlockSpec((n_seg, H_C, DK_C, DV_C), lambda b, t: (b, 0, 0, 0))]
    weights = [w["n1"], w["wgu1"], w["wd1"], w["nmix"], w["win"], w["wcat"],
               w["bias"], w["cw"], w["wg2"], w["bg2"], w["gn"], w["wo"]]
    in_specs = [x_spec] + (state_specs if has_init else []) + [_const_spec(a.shape) for a in weights]
    inputs = [x] + ([conv0, gla0] if has_init else []) + weights

    out_shape = [jax.ShapeDtypeStruct(x.shape, F32),
                 jax.ShapeDtypeStruct((nseq, CONV_W - 1, D_B), F32),
                 jax.ShapeDtypeStruct((nseq, H_C, DK_C, DV_C), F32)]
    out_specs = [x_spec] + state_specs
    if emit_vrows:
        out_shape.append(jax.ShapeDtypeStruct((nb, tlen, D_A), F32))
        out_specs.append(pl.BlockSpec((None, tm, D_A), lambda b, t: (b, t, 0)))

    scratch = [pltpu.VMEM((tm, D_MODEL), BF16),
               pltpu.VMEM((tm, D_MODEL), F32),
               pltpu.VMEM((tm, P_PAD), F32),
               pltpu.VMEM((tm, D_MODEL), BF16),
               pltpu.VMEM((tm, D_QK), F32),
               pltpu.VMEM((D_C, D_QK), F32),
               pltpu.VMEM((HIST_ROW + seg_len, D_B), F32),
               pltpu.VMEM((tm, tm), BF16)]
    kern = functools.partial(_mix_kernel, seg_len=seg_len, n_seg=n_seg, nt=nt, has_init=has_init,
                             emit_vrows=emit_vrows)
    return pl.pallas_call(
        kern, grid=(nb, nt), in_specs=in_specs, out_specs=out_specs, out_shape=out_shape,
        scratch_shapes=scratch, name="mix_layer",
        compiler_params=pltpu.CompilerParams(dimension_semantics=("arbitrary", "arbitrary"),
                                             vmem_limit_bytes=VMEM_LIMIT),
    )(*inputs)


def _ple_call(x, p, w, final_norm, *, tm, final):
    nb, tlen, _ = x.shape
    nt = tlen // tm
    x_spec = pl.BlockSpec((None, tm, D_MODEL), lambda b, t: (b, t, 0))
    p_spec = pl.BlockSpec((None, tm, D_PLE), lambda b, t: (b, t, 0))
    weights = [w["n2"], w["wgu2"], w["wd2"], w["nple"], w["wpg"], w["wpp"], final_norm]
    return pl.pallas_call(
        functools.partial(_ple_kernel, final=final), grid=(nb, nt),
        in_specs=[x_spec, p_spec] + [_const_spec(a.shape) for a in weights],
        out_specs=x_spec, out_shape=jax.ShapeDtypeStruct(x.shape, F32),
        scratch_shapes=[pltpu.VMEM((tm, D_MODEL), BF16), pltpu.VMEM((tm, D_MODEL), F32)],
        name="ple_layer",
        compiler_params=pltpu.CompilerParams(dimension_semantics=("arbitrary", "arbitrary"),
                                             vmem_limit_bytes=VMEM_LIMIT),
    )(x, p, *weights)


def _chunk_ffn(wg, wu, wd):
    g = wg.reshape(D_MODEL, N_FF, FF_CHUNK).transpose(1, 0, 2)
    u = wu.reshape(D_MODEL, N_FF, FF_CHUNK).transpose(1, 0, 2)
    return (jnp.concatenate([g, u], axis=-1).astype(BF16),
            wd.reshape(N_FF, FF_CHUNK, D_MODEL).astype(BF16))


def _layer_weights(i, lg, norm_ffn1, w_ffn1_gate, w_ffn1_up, w_ffn1_down, norm_mix, w_in, w_s, b_s, conv_w,
                   w_gate2, b_gate2, gla_norm, w_o, norm_ffn2, w_ffn2_gate, w_ffn2_up, w_ffn2_down, norm_ple,
                   w_ple_gate, w_ple_proj):
    wgu1, wd1 = _chunk_ffn(w_ffn1_gate[i], w_ffn1_up[i], w_ffn1_down[i])
    wgu2, wd2 = _chunk_ffn(w_ffn2_gate[i], w_ffn2_up[i], w_ffn2_down[i])
    row = lambda a: a.reshape(1, -1)
    return {
        "n1": row(norm_ffn1[i]), "wgu1": wgu1, "wd1": wd1, "nmix": row(norm_mix[i]),
        "win": jnp.pad(w_in[i], ((0, 0), (0, P_PAD - w_in.shape[-1]))).astype(BF16),
        "wcat": w_s[i, :, :lg, :lg].transpose(1, 0, 2).reshape(lg, H_A * lg),
        "bias": jnp.repeat(b_s[i, :, :lg].T, D_A // H_A, axis=1),
        "cw": conv_w[i],
        "wg2": jnp.pad(w_gate2[i], ((0, LANE - GATE_RANK), (0, 0))).astype(BF16),
        "bg2": row(b_gate2[i]), "gn": row(gla_norm[i]), "wo": w_o[i].astype(BF16),
        "n2": row(norm_ffn2[i]), "wgu2": wgu2, "wd2": wd2, "nple": row(norm_ple[i]),
        "wpg": w_ple_gate[i].astype(BF16), "wpp": w_ple_proj[i].astype(BF16),
    }


TM_PROMPT = 512


def kernel(x_prompt, x_sample, p_prompt, p_sample, state_conv, state_gla, norm_ffn1, w_ffn1_gate, w_ffn1_up,
           w_ffn1_down, norm_mix, w_in, w_s, b_s, conv_w, w_gate2, b_gate2, gla_norm, w_o, norm_ffn2,
           w_ffn2_gate, w_ffn2_up, w_ffn2_down, norm_ple, w_ple_gate, w_ple_proj, final_norm):
    depth = w_in.shape[0]
    dec_b, dec_t, _ = x_sample.shape
    params = (norm_ffn1, w_ffn1_gate, w_ffn1_up, w_ffn1_down, norm_mix, w_in, w_s, b_s, conv_w, w_gate2, b_gate2,
              gla_norm, w_o, norm_ffn2, w_ffn2_gate, w_ffn2_up, w_ffn2_down, norm_ple, w_ple_gate, w_ple_proj)
    fin = final_norm.reshape(1, -1)

    xp = x_prompt
    xs = x_sample.reshape(1, dec_b * dec_t, D_MODEL)
    ps = p_sample.reshape(depth, 1, dec_b * dec_t, D_PLE)
    conv_p, gla_p, conv_s, gla_s, vrow_s = [], [], [], [], []
    for i in range(depth):
        last = i == depth - 1
        wp = _layer_weights(i, min(TM_PROMPT, GMLP_CHUNK), *params)
        ws = _layer_weights(i, min(dec_t, GMLP_CHUNK), *params)

        xp, cp, sp = _mix_call(xp, None, None, wp, seg_len=TM_PROMPT, n_seg=1, emit_vrows=False)
        xp = _ple_call(xp, p_prompt[i], wp, fin, tm=TM_PROMPT, final=last)

        xs, cs, ss, vs = _mix_call(xs, state_conv[i], state_gla[i], ws, seg_len=dec_t, n_seg=dec_b,
                                   emit_vrows=True)
        xs = _ple_call(xs, ps[i], ws, fin, tm=dec_b * dec_t, final=last)

        conv_p.append(cp)
        gla_p.append(sp)
        conv_s.append(cs)
        gla_s.append(ss)
        vrow_s.append(vs.reshape(dec_b, dec_t, D_A))

    return (xp, xs.reshape(x_sample.shape), jnp.stack(conv_p), jnp.stack(gla_p), jnp.stack(conv_s),
            jnp.stack(gla_s), jnp.stack(vrow_s))
```

```python
import functools

import jax
import jax.numpy as jnp
from jax import lax
from jax.experimental import pallas as pl
from jax.experimental.pallas import tpu as pltpu

F32 = jnp.float32
BF16 = jnp.bfloat16

D_MODEL = 1024
D_FF = 2816
D_PLE = 256
D_A = 256
H_A = 4
GMLP_CHUNK = 128
CHUNK = 64
D_B = 256
CONV_W = 3
H_C = 4
DK_C = 64
DV_C = 128
D_QK = H_C * DK_C
D_C = H_C * DV_C
GATE_RANK = 16
GATE_TAU = 16.0
EPS = 1e-6

OFF_U, OFF_V, OFF_BB, OFF_CB, OFF_XB = 0, 256, 512, 768, 1024
OFF_Q, OFF_K, OFF_VC, OFF_R, OFF_GL = 1280, 1536, 1792, 2304, 2816
LANE = 128
P_PAD = OFF_GL + LANE
FF_CHUNK = 256
N_FF = D_FF // FF_CHUNK
HIST_ROW = 8
VMEM_LIMIT = 60 * 1024 * 1024


def _rms(x, g):
    ms = jnp.mean(x * x, axis=-1, keepdims=True)
    return (x * lax.rsqrt(ms + EPS)) * g


def _silu(x):
    return x * jax.nn.sigmoid(x)


def _log_sigmoid(z):
    return -(jnp.maximum(-z, 0.0) + jnp.log1p(jnp.exp(-jnp.abs(z))))


def _ffn_steps(x_ref, norm_ref, wg_ref, wu_ref, wd_ref, xn_ref, acc_ref):
    def prologue():
        xn_ref[...] = _rms(x_ref[...], norm_ref[...]).astype(BF16)
        acc_ref[...] = jnp.zeros_like(acc_ref)

    def chunk(c):
        def run():
            cols = slice(c * FF_CHUNK, (c + 1) * FF_CHUNK)
            wgu = jnp.concatenate([wg_ref[:, cols], wu_ref[:, cols]], axis=1)
            gu = jnp.dot(xn_ref[...], wgu, preferred_element_type=F32)
            a = (_silu(gu[:, :FF_CHUNK]) * gu[:, FF_CHUNK:]).astype(BF16)
            acc_ref[...] += jnp.dot(a, wd_ref[cols, :], preferred_element_type=F32)
        return run

    return [prologue] + [chunk(c) for c in range(N_FF)]


def _interleave(primary, secondary):
    order, j = [], 0
    for i, step in enumerate(primary):
        while j * len(primary) <= i * len(secondary) and j < len(secondary):
            order.append(secondary[j])
            j += 1
        order.append(step)
    return order + secondary[j:]


def _head_blocks(x, n_heads, width):
    head = lax.broadcasted_iota(jnp.int32, x.shape, 1) // width
    return jnp.concatenate([jnp.where(head == h, x, 0.0) for h in range(n_heads)], axis=0)


def _mix_kernel(*refs, seg_len, n_seg, nt, has_init, emit_vrows, pipelined):
    it = iter(refs)
    x_ref = next(it)
    conv0_ref = next(it) if has_init else None
    gla0_ref = next(it) if has_init else None
    n1_ref, wg_ref, wu_ref, wd_ref, nmix_ref, win_ref = (next(it) for _ in range(6))
    wcat_ref, bias_ref, cw_ref, wg2_ref, bg2_ref, gn_ref, wo_ref = (next(it) for _ in range(7))
    xo_ref, convo_ref, glao_ref = next(it), next(it), next(it)
    vrows_ref = next(it) if emit_vrows else None
    xn_ref, acc_ref, h_ref, x1_ref, y_ref, g_ref, st_ref, zp_ref = (next(it) for _ in range(8))

    n = pl.program_id(0)
    lg = min(seg_len, GMLP_CHUNK)
    lb = min(seg_len, CHUNK)

    if pipelined:
        @pl.when(n == 0)
        def _():
            h_ref[...] = jnp.zeros_like(h_ref)
            x1_ref[...] = jnp.zeros_like(x1_ref)
            st_ref[...] = jnp.zeros_like(st_ref)
            zp_ref[...] = jnp.zeros_like(zp_ref)

    ffn = _ffn_steps(x_ref, n1_ref, wg_ref, wu_ref, wd_ref, xn_ref, acc_ref)

    def in_proj():
        x1 = x_ref[...] + 0.5 * acc_ref[...]
        x1_ref[...] = x1
        xn_ref[...] = _rms(x1, nmix_ref[...]).astype(BF16)
        h_ref[...] = jnp.dot(xn_ref[...], win_ref[...], preferred_element_type=F32)

    fresh = (jnp.maximum(n - 1, 0) % nt == 0) if pipelined else (n % nt == 0)

    def gate():
        z = jnp.dot(h_ref[:, OFF_GL:OFF_GL + LANE].astype(BF16), wg2_ref[...], preferred_element_type=F32)
        g = _log_sigmoid(z + bg2_ref[...]) * (1.0 / GATE_TAU)
        row = lax.broadcasted_iota(jnp.int32, g.shape, 0) % lb
        shift = 1
        while shift < lb:
            g = g + jnp.where(row >= shift, pltpu.roll(g, shift, axis=0), 0.0)
            shift *= 2
        g_ref[...] = g

    def segment_steps(s, r0):
        def init_state():
            if has_init:
                zp_ref[HIST_ROW - 2:HIST_ROW, :] = conv0_ref[s]
                zero = jnp.zeros((DK_C, DV_C), F32)
                for p in range(H_C // 2):
                    st_ref[p] = jnp.concatenate(
                        [jnp.concatenate([gla0_ref[s, 2 * p], zero], axis=1),
                         jnp.concatenate([zero, gla0_ref[s, 2 * p + 1]], axis=1)], axis=0)
            else:
                zp_ref[HIST_ROW - 2:HIST_ROW, :] = jnp.where(fresh, 0.0, zp_ref[HIST_ROW - 2:HIST_ROW, :])
                st_ref[...] = jnp.where(fresh, 0.0, st_ref[...])

        def gmlp_chunk(c):
            def run():
                wi = lax.broadcasted_iota(jnp.int32, (lg, H_A * lg), 0)
                wj = lax.broadcasted_iota(jnp.int32, (lg, H_A * lg), 1) % lg
                wmask = jnp.where(wj // CHUNK <= wi // CHUNK, wcat_ref[...], 0.0).astype(BF16)
                rows = pl.ds(pl.multiple_of(r0 + c * lg, lg), lg)
                v = h_ref[rows, OFF_V:OFF_V + D_A]
                mu = jnp.mean(v, axis=-1, keepdims=True)
                vc = v - mu
                var = jnp.mean(vc * vc, axis=-1, keepdims=True)
                vn = vc * lax.rsqrt(var + EPS)
                if emit_vrows:
                    vrows_ref[rows, :] = vn
                vbd = _head_blocks(vn, H_A, D_A // H_A).astype(BF16)
                sp = jnp.dot(wmask, vbd, preferred_element_type=F32) + bias_ref[...]
                y_ref[rows, 0:D_A] = (h_ref[rows, OFF_U:OFF_U + D_A] * sp).astype(BF16)
            return run

        def conv():
            seg = pl.ds(r0, seg_len)
            zc = h_ref[seg, OFF_CB:OFF_CB + D_B] * h_ref[seg, OFF_XB:OFF_XB + D_B]
            zp_ref[HIST_ROW:HIST_ROW + seg_len, :] = zc
            yb = (cw_ref[0:1, :] * zp_ref[HIST_ROW - 2:HIST_ROW - 2 + seg_len, :]
                  + cw_ref[1:2, :] * zp_ref[HIST_ROW - 1:HIST_ROW - 1 + seg_len, :]
                  + cw_ref[2:3, :] * zc)
            y_ref[seg, D_A:D_A + D_B] = (h_ref[seg, OFF_BB:OFF_BB + D_B] * yb).astype(BF16)
            hist = zp_ref[HIST_ROW + seg_len - 2:HIST_ROW + seg_len, :]
            zp_ref[HIST_ROW - 2:HIST_ROW, :] = hist
            convo_ref[s] = hist

        def gla_block(b):
            def run():
                ai = lax.broadcasted_iota(jnp.int32, (lb, 2 * lb), 0)
                aj = lax.broadcasted_iota(jnp.int32, (lb, 2 * lb), 1) % lb
                sr = lax.broadcasted_iota(jnp.int32, (2 * DK_C, 2 * DV_C), 0) // DK_C
                sc = lax.broadcasted_iota(jnp.int32, (2 * DK_C, 2 * DV_C), 1) // DV_C
                rows = pl.ds(pl.multiple_of(r0 + b * lb, lb), lb)
                gcum = g_ref[rows, :]
                glast = gcum[lb - 1:lb, :]
                q = h_ref[rows, OFF_Q:OFF_Q + D_QK] * (DK_C ** -0.5)
                k = h_ref[rows, OFF_K:OFF_K + D_QK]
                q_t = (q * jnp.exp(gcum)).astype(BF16)
                k_t = k * jnp.exp(-gcum)
                k_e = (k * jnp.exp(glast - gcum)).astype(BF16)
                decay = jnp.exp(glast)
                for p in range(H_C // 2):
                    qk = slice(2 * p * DK_C, 2 * (p + 1) * DK_C)
                    v = h_ref[rows, OFF_VC + 2 * p * DV_C:OFF_VC + 2 * (p + 1) * DV_C]
                    kbd = _head_blocks(k_t[:, qk], 2, DK_C).astype(BF16)
                    a = lax.dot_general(q_t[:, qk], kbd, (((1,), (1,)), ((), ())), preferred_element_type=F32)
                    a = jnp.where(aj <= ai, a, 0.0).astype(BF16)
                    vbd = _head_blocks(v, 2, DV_C).astype(BF16)
                    st = st_ref[p]
                    o = jnp.dot(a, vbd, preferred_element_type=F32)
                    o = o + jnp.dot(q_t[:, qk], st.astype(BF16), preferred_element_type=F32)
                    ds = lax.dot_general(k_e[:, qk], v.astype(BF16), (((0,), (0,)), ((), ())),
                                         preferred_element_type=F32)
                    dcol = jnp.broadcast_to(decay[:, qk], (2 * DK_C, 2 * DK_C)).T
                    st_ref[p] = st * jnp.concatenate([dcol] * (DV_C // DK_C), axis=1) + jnp.where(sr == sc, ds, 0.0)
                    for hh in range(2):
                        h = 2 * p + hh
                        oh = o[:, hh * DV_C:(hh + 1) * DV_C]
                        oh = oh * lax.rsqrt(jnp.mean(oh * oh, axis=-1, keepdims=True) + EPS)
                        oh = oh * gn_ref[:, h * DV_C:(h + 1) * DV_C]
                        r = h_ref[rows, OFF_R + h * DV_C:OFF_R + (h + 1) * DV_C]
                        y_ref[rows, D_A + D_B + h * DV_C:D_A + D_B + (h + 1) * DV_C] = (oh * _silu(r)).astype(BF16)
            return run

        def state_out():
            for h in range(H_C):
                hh = h % 2
                glao_ref[s, h] = st_ref[h // 2, hh * DK_C:(hh + 1) * DK_C, hh * DV_C:(hh + 1) * DV_C]

        return ([init_state] + [gmlp_chunk(c) for c in range(seg_len // lg)] + [conv]
                + [gla_block(b) for b in range(seg_len // lb)] + [state_out])

    def out_proj():
        xo_ref[...] = x1_ref[...] + jnp.dot(y_ref[...], wo_ref[...], preferred_element_type=F32)

    if pipelined:
        assert n_seg == 1
        order = _interleave(ffn, [gate] + segment_steps(0, 0) + [out_proj]) + [in_proj]
    else:
        def all_segments():
            def body(s, carry):
                for step in segment_steps(s, pl.multiple_of(s * seg_len, seg_len)):
                    step()
                return carry
            lax.fori_loop(0, n_seg, body, 0)
        order = ffn + [in_proj, gate, all_segments, out_proj]
    for step in order:
        step()


def _ple_kernel(x_ref, p_ref, n2_ref, wg_ref, wu_ref, wd_ref, nple_ref, wpg_ref, wpp_ref, fin_ref, xo_ref,
                xn_ref, acc_ref, *, final):
    for step in _ffn_steps(x_ref, n2_ref, wg_ref, wu_ref, wd_ref, xn_ref, acc_ref):
        step()
    x3 = x_ref[...] + 0.5 * acc_ref[...]
    xn = _rms(x3, nple_ref[...]).astype(BF16)
    gate = jax.nn.sigmoid(jnp.dot(xn, wpg_ref[...], preferred_element_type=F32))
    proj = jnp.dot(p_ref[...].astype(BF16), wpp_ref[...], preferred_element_type=F32)
    x4 = x3 + gate * proj
    xo_ref[...] = _rms(x4, fin_ref[...]) if final else x4


def _const_spec(shape):
    nd = len(shape)
    return pl.BlockSpec(shape, lambda *_: (0,) * nd, pipeline_mode=pl.Buffered(1))


def _mix_call(x, conv0, gla0, w, *, seg_len, n_seg, emit_vrows, pipelined):
    nb, tlen, _ = x.shape
    tm = seg_len * n_seg
    nt = tlen // tm
    n_tiles = nb * nt
    nseq = nb * n_seg
    has_init = conv0 is not None

    def tile_in(n):
        return jnp.minimum(n, n_tiles - 1)

    def tile_out(n):
        return jnp.maximum(n - 1, 0) if pipelined else n

    x_in_spec = pl.BlockSpec((None, tm, D_MODEL), lambda n: (tile_in(n) // nt, tile_in(n) % nt, 0))
    x_out_spec = pl.BlockSpec((None, tm, D_MODEL), lambda n: (tile_out(n) // nt, tile_out(n) % nt, 0))
    state_specs = [pl.BlockSpec((n_seg, CONV_W - 1, D_B), lambda n: (tile_out(n) // nt, 0, 0)),
                   pl.BlockSpec((n_seg, H_C, DK_C, DV_C), lambda n: (tile_out(n) // nt, 0, 0, 0))]
    weights = [w["n1"], w["wg1"], w["wu1"], w["wd1"], w["nmix"], w["win"], w["wcat"],
               w["bias"], w["cw"], w["wg2"], w["bg2"], w["gn"], w["wo"]]
    in_specs = [x_in_spec] + (state_specs if has_init else []) + [_const_spec(a.shape) for a in weights]
    inputs = [x] + ([conv0, gla0] if has_init else []) + weights

    out_shape = [jax.ShapeDtypeStruct(x.shape, F32),
                 jax.ShapeDtypeStruct((nseq, CONV_W - 1, D_B), F32),
                 jax.ShapeDtypeStruct((nseq, H_C, DK_C, DV_C), F32)]
    out_specs = [x_out_spec] + state_specs
    if emit_vrows:
        out_shape.append(jax.ShapeDtypeStruct((nb, tlen, D_A), F32))
        out_specs.append(pl.BlockSpec((None, tm, D_A), lambda n: (tile_out(n) // nt, tile_out(n) % nt, 0)))

    scratch = [pltpu.VMEM((tm, D_MODEL), BF16),
               pltpu.VMEM((tm, D_MODEL), F32),
               pltpu.VMEM((tm, P_PAD), F32),
               pltpu.VMEM((tm, D_MODEL), F32),
               pltpu.VMEM((tm, D_MODEL), BF16),
               pltpu.VMEM((tm, D_QK), F32),
               pltpu.VMEM((H_C // 2, 2 * DK_C, 2 * DV_C), F32),
               pltpu.VMEM((HIST_ROW + seg_len, D_B), F32)]
    kern = functools.partial(_mix_kernel, seg_len=seg_len, n_seg=n_seg, nt=nt, has_init=has_init,
                             emit_vrows=emit_vrows, pipelined=pipelined)
    return pl.pallas_call(
        kern, grid=(n_tiles + 1 if pipelined else n_tiles,), in_specs=in_specs, out_specs=out_specs,
        out_shape=out_shape, scratch_shapes=scratch, name="mix_layer",
        compiler_params=pltpu.CompilerParams(dimension_semantics=("arbitrary",), vmem_limit_bytes=VMEM_LIMIT),
    )(*inputs)


def _ple_call(x, p, layer, w, final_norm, *, tm, final):
    nb, tlen, _ = x.shape
    nt = tlen // tm
    x_spec = pl.BlockSpec((None, tm, D_MODEL), lambda b, t: (b, t, 0))
    p_spec = pl.BlockSpec((None, None, tm, D_PLE), lambda b, t: (layer, b, t, 0))
    weights = [w["n2"], w["wg2f"], w["wu2"], w["wd2"], w["nple"], w["wpg"], w["wpp"], final_norm]
    return pl.pallas_call(
        functools.partial(_ple_kernel, final=final), grid=(nb, nt),
        in_specs=[x_spec, p_spec] + [_const_spec(a.shape) for a in weights],
        out_specs=x_spec, out_shape=jax.ShapeDtypeStruct(x.shape, F32),
        scratch_shapes=[pltpu.VMEM((tm, D_MODEL), BF16), pltpu.VMEM((tm, D_MODEL), F32)],
        name="ple_layer",
        compiler_params=pltpu.CompilerParams(dimension_semantics=("arbitrary", "arbitrary"),
                                             vmem_limit_bytes=VMEM_LIMIT),
    )(x, p, *weights)


def _layer_weights(i, lg, norm_ffn1, w_ffn1_gate, w_ffn1_up, w_ffn1_down, norm_mix, w_in, w_s, b_s, conv_w,
                   w_gate2, b_gate2, gla_norm, w_o, norm_ffn2, w_ffn2_gate, w_ffn2_up, w_ffn2_down, norm_ple,
                   w_ple_gate, w_ple_proj):
    row = lambda a: a.reshape(1, -1)
    return {
        "n1": row(norm_ffn1[i]), "nmix": row(norm_mix[i]),
        "wg1": w_ffn1_gate[i].astype(BF16), "wu1": w_ffn1_up[i].astype(BF16), "wd1": w_ffn1_down[i].astype(BF16),
        "win": jnp.pad(w_in[i], ((0, 0), (0, P_PAD - w_in.shape[-1]))).astype(BF16),
        "wcat": w_s[i, :, :lg, :lg].transpose(1, 0, 2).reshape(lg, H_A * lg),
        "bias": jnp.repeat(b_s[i, :, :lg].T, D_A // H_A, axis=1),
        "cw": conv_w[i],
        "wg2": jnp.pad(w_gate2[i], ((0, LANE - GATE_RANK), (0, 0))).astype(BF16),
        "bg2": row(b_gate2[i]), "gn": row(gla_norm[i]), "wo": w_o[i].astype(BF16),
        "n2": row(norm_ffn2[i]), "nple": row(norm_ple[i]),
        "wg2f": w_ffn2_gate[i].astype(BF16), "wu2": w_ffn2_up[i].astype(BF16), "wd2": w_ffn2_down[i].astype(BF16),
        "wpg": w_ple_gate[i].astype(BF16), "wpp": w_ple_proj[i].astype(BF16),
    }


TM_PROMPT = 512


def kernel(x_prompt, x_sample, p_prompt, p_sample, state_conv, state_gla, norm_ffn1, w_ffn1_gate, w_ffn1_up,
           w_ffn1_down, norm_mix, w_in, w_s, b_s, conv_w, w_gate2, b_gate2, gla_norm, w_o, norm_ffn2,
           w_ffn2_gate, w_ffn2_up, w_ffn2_down, norm_ple, w_ple_gate, w_ple_proj, final_norm):
    depth = w_in.shape[0]
    dec_b, dec_t, _ = x_sample.shape
    params = (norm_ffn1, w_ffn1_gate, w_ffn1_up, w_ffn1_down, norm_mix, w_in, w_s, b_s, conv_w, w_gate2, b_gate2,
              gla_norm, w_o, norm_ffn2, w_ffn2_gate, w_ffn2_up, w_ffn2_down, norm_ple, w_ple_gate, w_ple_proj)
    fin = final_norm.reshape(1, -1)

    xp = x_prompt
    xs = x_sample.reshape(1, dec_b * dec_t, D_MODEL)
    ps = p_sample.reshape(depth, 1, dec_b * dec_t, D_PLE)
    conv_p, gla_p, conv_s, gla_s, vrow_s = [], [], [], [], []
    for i in range(depth):
        last = i == depth - 1
        wp = _layer_weights(i, min(TM_PROMPT, GMLP_CHUNK), *params)
        ws = _layer_weights(i, min(dec_t, GMLP_CHUNK), *params)

        xp, cp, sp = _mix_call(xp, None, None, wp, seg_len=TM_PROMPT, n_seg=1, emit_vrows=False, pipelined=True)
        xp = _ple_call(xp, p_prompt, i, wp, fin, tm=TM_PROMPT, final=last)

        xs, cs, ss, vs = _mix_call(xs, state_conv[i], state_gla[i], ws, seg_len=dec_t, n_seg=dec_b,
                                   emit_vrows=True, pipelined=False)
        xs = _ple_call(xs, ps, i, ws, fin, tm=dec_b * dec_t, final=last)

        conv_p.append(cp)
        gla_p.append(sp)
        conv_s.append(cs)
        gla_s.append(ss)
        vrow_s.append(vs.reshape(dec_b, dec_t, D_A))

    return (xp, xs.reshape(x_sample.shape), jnp.stack(conv_p), jnp.stack(gla_p), jnp.stack(conv_s),
            jnp.stack(gla_s), jnp.stack(vrow_s))
```
